```python
import jax, jax.numpy as jnp
from jax import lax
import numpy as np

D_MODEL = 1024
BATCH = 8
SEQ = 4096
DEPTH = 1

D_MIX = D_MODEL
SC_WIDTH = D_MIX // 2
LRU_WIDTH = D_MIX - SC_WIDTH
SC_GROUPS = 8
LRU_HEADS = 8
LRU_HEAD_DIM = LRU_WIDTH // LRU_HEADS
SC_CONV_W = 3
LRU_CONV_W = 4
RG_C = 8.0
D_FF = 4 * D_MODEL
N_MOD = 6
EPS = 1e-6
D_IN = 3 * SC_WIDTH + 2 * LRU_WIDTH

kernel_name = "hybrid_shortconv_rglru_adaln_block"


def rmsnorm(x, g):
    xf = x.astype(jnp.float32)
    y = xf * lax.rsqrt(jnp.mean(xf * xf, axis=-1, keepdims=True) + EPS)
    return y.astype(x.dtype) * g


def causal_dwconv(u, w):
    k_w = w.shape[0]
    s = u.shape[1]
    up = jnp.pad(u, ((0, 0), (k_w - 1, 0), (0, 0)))
    out = up[:, 0:s] * w[0]
    for k in range(1, k_w):
        out = out + up[:, k:k + s] * w[k]
    return out


def _lru_combine(left, right):
    a_l, b_l = left
    a_r, b_r = right
    return a_l * a_r, a_r * b_l + b_r


def rg_lru(u, w_a, b_a, w_x, b_x, lam):
    bsz, s, _ = u.shape
    uh = u.reshape(bsz, s, LRU_HEADS, LRU_HEAD_DIM)
    r = jax.nn.sigmoid(jnp.einsum('bshi,hij->bshj', uh, w_a) + b_a)
    i = jax.nn.sigmoid(jnp.einsum('bshi,hij->bshj', uh, w_x) + b_x)
    r32 = r.astype(jnp.float32)
    log_a = -RG_C * r32 * jax.nn.softplus(-lam.astype(jnp.float32).reshape(LRU_HEADS, LRU_HEAD_DIM))
    a = jnp.exp(log_a)
    mult = jnp.sqrt(-jnp.expm1(2.0 * log_a))
    b = mult * (i * uh).astype(jnp.float32)
    _, h = lax.associative_scan(_lru_combine, (a, b), axis=1)
    return h.astype(u.dtype).reshape(bsz, s, LRU_WIDTH)


def setup_inputs(seed: int = 0) -> dict:
    key = jax.random.key(seed)
    ks = jax.random.split(key, 20)
    f32 = jnp.float32
    nrm = lambda k, shape, scale: jax.random.normal(k, shape, f32) * scale
    u = jax.random.uniform(ks[13], (DEPTH, LRU_WIDTH), f32, 0.9, 0.999)
    a0 = u ** (1.0 / RG_C)
    lam = jnp.log(a0) - jnp.log1p(-a0)
    return {
        "x": nrm(ks[0], (BATCH, SEQ, D_MODEL), 1.0),
        "c": nrm(ks[1], (BATCH, D_MODEL), 1.0),
        "w_ada": nrm(ks[2], (DEPTH, D_MODEL, N_MOD * D_MODEL), 0.5 * D_MODEL ** -0.5),
        "b_ada": nrm(ks[3], (DEPTH, N_MOD * D_MODEL), 0.01),
        "g_mix": 1.0 + nrm(ks[4], (DEPTH, D_MODEL), 0.01),
        "w_in": nrm(ks[5], (DEPTH, D_MODEL, D_IN), D_MODEL ** -0.5),
        "conv_w_sc": nrm(ks[6], (DEPTH, SC_CONV_W, SC_WIDTH), SC_CONV_W ** -0.5),
        "conv_w_lru": nrm(ks[7], (DEPTH, LRU_CONV_W, LRU_WIDTH), LRU_CONV_W ** -0.5),
        "conv_b_lru": nrm(ks[8], (DEPTH, LRU_WIDTH), 0.01),
        "w_rg_a": nrm(ks[9], (DEPTH, LRU_HEADS, LRU_HEAD_DIM, LRU_HEAD_DIM), LRU_HEAD_DIM ** -0.5),
        "b_rg_a": nrm(ks[10], (DEPTH, LRU_HEADS, LRU_HEAD_DIM), 0.01),
        "w_rg_x": nrm(ks[11], (DEPTH, LRU_HEADS, LRU_HEAD_DIM, LRU_HEAD_DIM), LRU_HEAD_DIM ** -0.5),
        "b_rg_x": nrm(ks[12], (DEPTH, LRU_HEADS, LRU_HEAD_DIM), 0.01),
        "lru_lambda": lam,
        "w_out": nrm(ks[14], (DEPTH, D_MIX, D_MODEL), D_MIX ** -0.5),
        "g_mlp": 1.0 + nrm(ks[15], (DEPTH, D_MODEL), 0.01),
        "w_up": nrm(ks[16], (DEPTH, D_MODEL, D_FF), D_MODEL ** -0.5),
        "w_down": nrm(ks[17], (DEPTH, D_FF, D_MODEL), D_FF ** -0.5),
        "g_final": 1.0 + nrm(ks[18], (D_MODEL,), 0.01),
    }


def reference(x, c, w_ada, b_ada, g_mix, w_in, conv_w_sc, conv_w_lru, conv_b_lru,
              w_rg_a, b_rg_a, w_rg_x, b_rg_x, lru_lambda, w_out, g_mlp, w_up, w_down,
              g_final):
    c_act = jax.nn.silu(c)
    for l in range(DEPTH):
        mod = c_act @ w_ada[l] + b_ada[l]
        sh_m, sc_m, gt_m, sh_f, sc_f, gt_f = [m[:, None, :] for m in jnp.split(mod, N_MOD, axis=-1)]

        hn = rmsnorm(x, g_mix[l]) * (1.0 + sc_m) + sh_m
        proj = hn @ w_in[l]
        o1 = SC_WIDTH
        o2 = 2 * SC_WIDTH
        o3 = 3 * SC_WIDTH
        o4 = 3 * SC_WIDTH + LRU_WIDTH
        sc_b, sc_c, sc_x = proj[..., :o1], proj[..., o1:o2], proj[..., o2:o3]
        lru_y, lru_x = proj[..., o3:o4], proj[..., o4:]

        y_sc = sc_b * causal_dwconv(sc_c * sc_x, conv_w_sc[l])

        u = causal_dwconv(lru_x, conv_w_lru[l]) + conv_b_lru[l]
        h = rg_lru(u, w_rg_a[l], b_rg_a[l], w_rg_x[l], b_rg_x[l], lru_lambda[l])
        y_lru = jax.nn.gelu(lru_y, approximate=True) * h

        mix = jnp.concatenate([y_sc, y_lru], axis=-1) @ w_out[l]
        x = x + gt_m * mix

        hn = rmsnorm(x, g_mlp[l]) * (1.0 + sc_f) + sh_f
        z = jax.nn.relu(hn @ w_up[l])
        x = x + gt_f * ((z * z) @ w_down[l])
    return rmsnorm(x, g_final)
```

```python
import functools

import jax
import jax.numpy as jnp
from jax import lax
from jax.experimental import pallas as pl
from jax.experimental.pallas import tpu as pltpu

D_MODEL = 1024
SC_WIDTH = 512
LRU_WIDTH = 512
LRU_HEADS = 8
LRU_HEAD_DIM = 64
SC_CONV_W = 3
LRU_CONV_W = 4
RG_C = 8.0
D_FF = 4096
N_MOD = 6
EPS = 1e-6
D_IN = 3 * SC_WIDTH + 2 * LRU_WIDTH

SUBLANES = 8
GATE_HALF = 256
SEQ_TILE = 512
ROW_CHUNK = 64
FF_CHUNK = 1024
VMEM_LIMIT_BYTES = 58 * 1024 * 1024

_f32 = jnp.float32
_bf16 = jnp.bfloat16


def _ada_kernel(c_ref, w_ref, b_ref, o_ref):
    c = c_ref[...]
    c_act = c * jax.nn.sigmoid(c)
    o_ref[...] = jnp.dot(c_act, w_ref[...], preferred_element_type=_f32) + b_ref[...]


def _adaln_mod(c, w_ada, b_ada):
    bsz = c.shape[0]
    return pl.pallas_call(
        _ada_kernel,
        grid=(N_MOD,),
        in_specs=[
            pl.BlockSpec((bsz, D_MODEL), lambda i: (0, 0)),
            pl.BlockSpec((D_MODEL, D_MODEL), lambda i: (0, i)),
            pl.BlockSpec((1, D_MODEL), lambda i: (0, i)),
        ],
        out_specs=pl.BlockSpec((bsz, D_MODEL), lambda i: (0, i)),
        out_shape=jax.ShapeDtypeStruct((bsz, N_MOD * D_MODEL), _f32),
        name="adaln_mod",
    )(c, w_ada, b_ada)


def _rms_scale(v):
    return lax.rsqrt(jnp.mean(v * v, axis=-1, keepdims=True) + EPS)


def _block_kernel(x_ref, mod_ref, gmix_ref, w_in_ref, cwsc_ref, cwlru_ref, cblru_ref,
                  wg_ref, bg_ref, lam_ref, w_out_ref, gmlp_ref, w_up_ref, w_down_ref,
                  gfin_ref, o_ref,
                  hn_ref, proj_ref, pbuf_ref, ubuf_ref, u_ref, ub_ref, y_ref, x1_ref,
                  z_ref, hcar_ref):
    t = SEQ_TILE
    j = pl.program_id(1)
    mod = mod_ref[...]
    sh_m, sc_m, gt_m, sh_f, sc_f, gt_f = [mod[i:i + 1, :] for i in range(N_MOD)]
    chunks = [(i * ROW_CHUNK, pl.ds(i * ROW_CHUNK, ROW_CHUNK)) for i in range(t // ROW_CHUNK)]

    @pl.when(j == 0)
    def _():
        pbuf_ref[0:SUBLANES, :] = jnp.zeros((SUBLANES, SC_WIDTH), _f32)
        ubuf_ref[0:SUBLANES, :] = jnp.zeros((SUBLANES, LRU_WIDTH), _f32)
        hcar_ref[...] = jnp.zeros_like(hcar_ref)

    @pl.when(j > 0)
    def _():
        pbuf_ref[0:SUBLANES, :] = pbuf_ref[t:t + SUBLANES, :]
        ubuf_ref[0:SUBLANES, :] = ubuf_ref[t:t + SUBLANES, :]

    scale_m = gmix_ref[...] * (1.0 + sc_m)
    for _, r in chunks:
        xv = x_ref[r, :]
        hn_ref[r, :] = (xv * _rms_scale(xv) * scale_m + sh_m).astype(_bf16)
    proj_ref[...] = jnp.dot(hn_ref[...], w_in_ref[...], preferred_element_type=_f32)

    o1, o2, o3, o4 = SC_WIDTH, 2 * SC_WIDTH, 3 * SC_WIDTH, 3 * SC_WIDTH + LRU_WIDTH

    for r0, r in chunks:
        pbuf_ref[pl.ds(SUBLANES + r0, ROW_CHUNK), :] = proj_ref[r, o1:o2] * proj_ref[r, o2:o3]
    cw = cwsc_ref[...]
    for r0, r in chunks:
        conv = cw[0:1, :] * pbuf_ref[pl.ds(SUBLANES - 2 + r0, ROW_CHUNK), :]
        conv = conv + cw[1:2, :] * pbuf_ref[pl.ds(SUBLANES - 1 + r0, ROW_CHUNK), :]
        conv = conv + cw[2:3, :] * pbuf_ref[pl.ds(SUBLANES + r0, ROW_CHUNK), :]
        y_ref[r, 0:SC_WIDTH] = (proj_ref[r, 0:o1] * conv).astype(_bf16)

    for r0, r in chunks:
        ubuf_ref[pl.ds(SUBLANES + r0, ROW_CHUNK), :] = proj_ref[r, o4:D_IN]
    cl = cwlru_ref[...]
    for r0, r in chunks:
        u = cblru_ref[...] + cl[0:1, :] * ubuf_ref[pl.ds(SUBLANES - 3 + r0, ROW_CHUNK), :]
        u = u + cl[1:2, :] * ubuf_ref[pl.ds(SUBLANES - 2 + r0, ROW_CHUNK), :]
        u = u + cl[2:3, :] * ubuf_ref[pl.ds(SUBLANES - 1 + r0, ROW_CHUNK), :]
        u = u + cl[3:4, :] * ubuf_ref[pl.ds(SUBLANES + r0, ROW_CHUNK), :]
        u_ref[r, :] = u
        ub_ref[r, :] = u.astype(_bf16)

    for half in range(2):
        c0 = half * GATE_HALF
        proj_ref[:, 2 * c0:2 * c0 + 2 * GATE_HALF] = jnp.dot(
            ub_ref[:, c0:c0 + GATE_HALF], wg_ref[half], preferred_element_type=_f32)

    neg_lam = -lam_ref[...]
    softplus = jnp.maximum(neg_lam, 0.0) + jnp.log1p(jnp.exp(-jnp.abs(neg_lam)))
    decay = -RG_C * softplus

    groups = ROW_CHUNK // SUBLANES
    sub = lax.broadcasted_iota(jnp.int32, (groups, SUBLANES, GATE_HALF), 1)
    for r0, r in chunks:
        for half in range(2):
            c0 = half * GATE_HALF
            g0 = 2 * c0
            bias = bg_ref[half]
            r_gate = jax.nn.sigmoid(proj_ref[r, g0:g0 + GATE_HALF] + bias[:, 0:GATE_HALF])
            i_gate = jax.nn.sigmoid(
                proj_ref[r, g0 + GATE_HALF:g0 + 2 * GATE_HALF] + bias[:, GATE_HALF:2 * GATE_HALF])
            log_a = r_gate * decay[:, c0:c0 + GATE_HALF]
            a = jnp.exp(log_a)
            mult = jnp.sqrt(-jnp.tanh(log_a) * (a * a + 1.0))
            b = mult * (i_gate * u_ref[r, c0:c0 + GATE_HALF])
            a3 = a.reshape(groups, SUBLANES, GATE_HALF)
            b3 = b.reshape(groups, SUBLANES, GATE_HALF)
            for d in (1, 2, 4):
                keep = sub >= d
                a_prev = pltpu.roll(a3, d, axis=1)
                b_prev = pltpu.roll(b3, d, axis=1)
                b3 = jnp.where(keep, a3 * b_prev + b3, b3)
                a3 = jnp.where(keep, a3 * a_prev, a3)
            proj_ref[r, o2 + c0:o2 + c0 + GATE_HALF] = a3.reshape(ROW_CHUNK, GATE_HALF)
            proj_ref[r, o4 + c0:o4 + c0 + GATE_HALF] = b3.reshape(ROW_CHUNK, GATE_HALF)

    def scan_body(g, h_in):
        rows = pl.ds(pl.multiple_of(g * SUBLANES, SUBLANES), SUBLANES)
        h = proj_ref[rows, o2:o3] * h_in + proj_ref[rows, o4:D_IN]
        proj_ref[rows, o4:D_IN] = h
        return jnp.broadcast_to(h[SUBLANES - 1:SUBLANES, :], (SUBLANES, LRU_WIDTH))

    hcar_ref[...] = lax.fori_loop(0, t // SUBLANES, scan_body, hcar_ref[...], unroll=8)

    for _, r in chunks:
        y_ref[r, SC_WIDTH:D_MODEL] = (
            jax.nn.gelu(proj_ref[r, o3:o4], approximate=True) * proj_ref[r, o4:D_IN]).astype(_bf16)

    x1_ref[...] = jnp.dot(y_ref[...], w_out_ref[...], preferred_element_type=_f32)
    scale_f = gmlp_ref[...] * (1.0 + sc_f)
    for _, r in chunks:
        x1 = x_ref[r, :] + gt_m * x1_ref[r, :]
        x1_ref[r, :] = x1
        hn_ref[r, :] = (x1 * _rms_scale(x1) * scale_f + sh_f).astype(_bf16)

    for c in range(D_FF // FF_CHUNK):
        cols = slice(c * FF_CHUNK, (c + 1) * FF_CHUNK)
        z = jnp.maximum(jnp.dot(hn_ref[...], w_up_ref[:, cols], preferred_element_type=_f32), 0.0)
        z_ref[:, cols] = (z * z).astype(_bf16)
    proj_ref[:, 0:D_MODEL] = jnp.dot(z_ref[...], w_down_ref[...], preferred_element_type=_f32)

    gfin = gfin_ref[...]
    for _, r in chunks:
        x2 = x1_ref[r, :] + gt_f * proj_ref[r, 0:D_MODEL]
        o_ref[r, :] = x2 * _rms_scale(x2) * gfin


def _gate_block_diag(w):
    per_half = GATE_HALF // LRU_HEAD_DIM
    w4 = w.reshape(2, per_half, LRU_HEAD_DIM, LRU_HEAD_DIM)
    eye = jnp.eye(per_half, dtype=w.dtype)
    return jnp.einsum('hiab,ij->hiajb', w4, eye).reshape(2, GATE_HALF, GATE_HALF)


def _const_spec(shape):
    zeros = (0,) * len(shape)
    return pl.BlockSpec(shape, lambda b, j: zeros, pipeline_mode=pl.Buffered(1))


def kernel(x, c, w_ada, b_ada, g_mix, w_in, conv_w_sc, conv_w_lru, conv_b_lru, w_rg_a, b_rg_a,
           w_rg_x, b_rg_x, lru_lambda, w_out, g_mlp, w_up, w_down, g_final):
    bsz, seq, _ = x.shape
    assert w_ada.shape[0] == 1 and seq % SEQ_TILE == 0
    t = SEQ_TILE

    mod = _adaln_mod(c, w_ada[0], b_ada[0].reshape(1, -1)).reshape(bsz, N_MOD, D_MODEL)

    wg = jnp.concatenate([_gate_block_diag(w_rg_a[0]), _gate_block_diag(w_rg_x[0])],
                         axis=-1).astype(_bf16)
    bg = jnp.concatenate([b_rg_a[0].reshape(2, 1, GATE_HALF), b_rg_x[0].reshape(2, 1, GATE_HALF)],
                         axis=-1)

    operands = (
        x, mod, g_mix, w_in[0].astype(_bf16), conv_w_sc[0], conv_w_lru[0], conv_b_lru,
        wg, bg, lru_lambda, w_out[0].astype(_bf16), g_mlp, w_up[0].astype(_bf16),
        w_down[0].astype(_bf16), g_final.reshape(1, D_MODEL),
    )
    in_specs = [
        pl.BlockSpec((None, t, D_MODEL), lambda b, j: (b, j, 0)),
        pl.BlockSpec((None, N_MOD, D_MODEL), lambda b, j: (b, 0, 0)),
    ] + [_const_spec(op.shape) for op in operands[2:]]

    return pl.pallas_call(
        _block_kernel,
        grid=(bsz, seq // t),
        in_specs=in_specs,
        out_specs=pl.BlockSpec((None, t, D_MODEL), lambda b, j: (b, j, 0)),
        out_shape=jax.ShapeDtypeStruct(x.shape, x.dtype),
        scratch_shapes=[
            pltpu.VMEM((t, D_MODEL), _bf16),
            pltpu.VMEM((t, D_IN), _f32),
            pltpu.VMEM((t + SUBLANES, SC_WIDTH), _f32),
            pltpu.VMEM((t + SUBLANES, LRU_WIDTH), _f32),
            pltpu.VMEM((t, LRU_WIDTH), _f32),
            pltpu.VMEM((t, LRU_WIDTH), _bf16),
            pltpu.VMEM((t, D_MODEL), _bf16),
            pltpu.VMEM((t, D_MODEL), _f32),
            pltpu.VMEM((t, D_FF), _bf16),
            pltpu.VMEM((SUBLANES, LRU_WIDTH), _f32),
        ],
        compiler_params=pltpu.CompilerParams(
            dimension_semantics=("arbitrary", "arbitrary"),
            vmem_limit_bytes=VMEM_LIMIT_BYTES,
        ),
        name="hybrid_block",
    )(*operands)
```

```python
import jax
import jax.numpy as jnp
from jax import lax
from jax.experimental import pallas as pl
from jax.experimental.pallas import tpu as pltpu

D_MODEL = 1024
SC_WIDTH = 512
LRU_WIDTH = 512
LRU_HEADS = 8
LRU_HEAD_DIM = 64
SC_CONV_W = 3
LRU_CONV_W = 4
RG_C = 8.0
D_FF = 4096
N_MOD = 6
EPS = 1e-6
D_IN = 3 * SC_WIDTH + 2 * LRU_WIDTH

SUBLANES = 8
GATE_HALF = 256
SEQ_TILE = 512
ROW_CHUNK = 64
FF_CHUNK = 1024
VMEM_LIMIT_BYTES = 58 * 1024 * 1024

_f32 = jnp.float32
_bf16 = jnp.bfloat16


def _ada_kernel(c_ref, w_ref, b_ref, o_ref):
    c = c_ref[...]
    c_act = c * jax.nn.sigmoid(c)
    o_ref[...] = jnp.dot(c_act, w_ref[...], preferred_element_type=_f32) + b_ref[...]


def _adaln_mod(c, w_ada, b_ada):
    bsz = c.shape[0]
    return pl.pallas_call(
        _ada_kernel,
        grid=(N_MOD,),
        in_specs=[
            pl.BlockSpec((bsz, D_MODEL), lambda i: (0, 0)),
            pl.BlockSpec((D_MODEL, D_MODEL), lambda i: (0, i)),
            pl.BlockSpec((1, D_MODEL), lambda i: (0, i)),
        ],
        out_specs=pl.BlockSpec((bsz, D_MODEL), lambda i: (0, i)),
        out_shape=jax.ShapeDtypeStruct((bsz, N_MOD * D_MODEL), _f32),
        name="adaln_mod",
    )(c, w_ada, b_ada)


def _rms_scale(v):
    return lax.rsqrt(jnp.mean(v * v, axis=-1, keepdims=True) + EPS)


def _shifted_rows(prev, cur, max_shift):
    groups = cur.shape[0] // SUBLANES
    width = cur.shape[1]
    ext = jnp.concatenate([prev, cur], axis=0).reshape(groups + 1, SUBLANES, width)
    sub = lax.broadcasted_iota(jnp.int32, (groups, SUBLANES, width), 1)
    out = {}
    for m in range(1, max_shift + 1):
        rot = pltpu.roll(ext, m, axis=1)
        out[m] = jnp.where(sub >= m, rot[1:], rot[:-1]).reshape(cur.shape)
    return out


def _block_kernel(x_ref, mod_ref, gmix_ref, w_in_ref, cwsc_ref, cwlru_ref, cblru_ref,
                  wg_ref, bg_ref, lam_ref, w_out_ref, gmlp_ref, w_up_ref, w_down_ref,
                  gfin_ref, o_ref,
                  hn_ref, proj_ref, ptail_ref, utail_ref, u_ref, ub_ref, y_ref, x1_ref,
                  z_ref, hcar_ref):
    t = SEQ_TILE
    j = pl.program_id(1)
    mod = mod_ref[...]
    sh_m, sc_m, gt_m, sh_f, sc_f, gt_f = [mod[i:i + 1, :] for i in range(N_MOD)]
    chunks = [(i * ROW_CHUNK, pl.ds(i * ROW_CHUNK, ROW_CHUNK)) for i in range(t // ROW_CHUNK)]

    @pl.when(j == 0)
    def _():
        ptail_ref[...] = jnp.zeros_like(ptail_ref)
        utail_ref[...] = jnp.zeros_like(utail_ref)
        hcar_ref[...] = jnp.zeros_like(hcar_ref)

    scale_m = gmix_ref[...] * (1.0 + sc_m)
    for _, r in chunks:
        xv = x_ref[r, :]
        hn_ref[r, :] = (xv * _rms_scale(xv) * scale_m + sh_m).astype(_bf16)
    proj_ref[...] = jnp.dot(hn_ref[...], w_in_ref[...], preferred_element_type=_f32)

    o1, o2, o3, o4 = SC_WIDTH, 2 * SC_WIDTH, 3 * SC_WIDTH, 3 * SC_WIDTH + LRU_WIDTH

    cw = cwsc_ref[...]
    prev = ptail_ref[...]
    for _, r in chunks:
        p = proj_ref[r, o1:o2] * proj_ref[r, o2:o3]
        back = _shifted_rows(prev, p, SC_CONV_W - 1)
        conv = cw[SC_CONV_W - 1:SC_CONV_W, :] * p
        for k in range(SC_CONV_W - 1):
            conv = conv + cw[k:k + 1, :] * back[SC_CONV_W - 1 - k]
        y_ref[r, 0:SC_WIDTH] = (proj_ref[r, 0:o1] * conv).astype(_bf16)
        prev = p[ROW_CHUNK - SUBLANES:, :]
    ptail_ref[...] = prev

    cl = cwlru_ref[...]
    prev = utail_ref[...]
    for _, r in chunks:
        lx = proj_ref[r, o4:D_IN]
        back = _shifted_rows(prev, lx, LRU_CONV_W - 1)
        u = cblru_ref[...] + cl[LRU_CONV_W - 1:LRU_CONV_W, :] * lx
        for k in range(LRU_CONV_W - 1):
            u = u + cl[k:k + 1, :] * back[LRU_CONV_W - 1 - k]
        u_ref[r, :] = u
        ub_ref[r, :] = u.astype(_bf16)
        prev = lx[ROW_CHUNK - SUBLANES:, :]
    utail_ref[...] = prev

    for half in range(2):
        c0 = half * GATE_HALF
        proj_ref[:, 2 * c0:2 * c0 + 2 * GATE_HALF] = jnp.dot(
            ub_ref[:, c0:c0 + GATE_HALF], wg_ref[half], preferred_element_type=_f32)

    neg_lam = -lam_ref[...]
    softplus = jnp.maximum(neg_lam, 0.0) + jnp.log1p(jnp.exp(-jnp.abs(neg_lam)))
    decay = -RG_C * softplus

    groups = ROW_CHUNK // SUBLANES
    sub = lax.broadcasted_iota(jnp.int32, (groups, SUBLANES, GATE_HALF), 1)
    for r0, r in chunks:
        for half in range(2):
            c0 = half * GATE_HALF
            g0 = 2 * c0
            bias = bg_ref[half]
            r_gate = jax.nn.sigmoid(proj_ref[r, g0:g0 + GATE_HALF] + bias[:, 0:GATE_HALF])
            i_gate = jax.nn.sigmoid(
                proj_ref[r, g0 + GATE_HALF:g0 + 2 * GATE_HALF] + bias[:, GATE_HALF:2 * GATE_HALF])
            log_a = r_gate * decay[:, c0:c0 + GATE_HALF]
            a = jnp.exp(log_a)
            mult = jnp.sqrt(-jnp.tanh(log_a) * (a * a + 1.0))
            b = mult * (i_gate * u_ref[r, c0:c0 + GATE_HALF])
            a3 = a.reshape(groups, SUBLANES, GATE_HALF)
            b3 = b.reshape(groups, SUBLANES, GATE_HALF)
            for d in (1, 2, 4):
                keep = sub >= d
                a_prev = pltpu.roll(a3, d, axis=1)
                b_prev = pltpu.roll(b3, d, axis=1)
                b3 = jnp.where(keep, a3 * b_prev + b3, b3)
                a3 = jnp.where(keep, a3 * a_prev, a3)
            proj_ref[r, o2 + c0:o2 + c0 + GATE_HALF] = a3.reshape(ROW_CHUNK, GATE_HALF)
            proj_ref[r, o4 + c0:o4 + c0 + GATE_HALF] = b3.reshape(ROW_CHUNK, GATE_HALF)

    def scan_body(g, h_in):
        rows = pl.ds(pl.multiple_of(g * SUBLANES, SUBLANES), SUBLANES)
        h = proj_ref[rows, o2:o3] * h_in + proj_ref[rows, o4:D_IN]
        proj_ref[rows, o4:D_IN] = h
        return jnp.broadcast_to(h[SUBLANES - 1:SUBLANES, :], (SUBLANES, LRU_WIDTH))

    hcar_ref[...] = lax.fori_loop(0, t // SUBLANES, scan_body, hcar_ref[...], unroll=8)

    for _, r in chunks:
        y_ref[r, SC_WIDTH:D_MODEL] = (
            jax.nn.gelu(proj_ref[r, o3:o4], approximate=True) * proj_ref[r, o4:D_IN]).astype(_bf16)

    x1_ref[...] = jnp.dot(y_ref[...], w_out_ref[...], preferred_element_type=_f32)
    scale_f = gmlp_ref[...] * (1.0 + sc_f)
    for _, r in chunks:
        x1 = x_ref[r, :] + gt_m * x1_ref[r, :]
        x1_ref[r, :] = x1
        hn_ref[r, :] = (x1 * _rms_scale(x1) * scale_f + sh_f).astype(_bf16)

    for c in range(D_FF // FF_CHUNK):
        cols = slice(c * FF_CHUNK, (c + 1) * FF_CHUNK)
        z = jnp.maximum(jnp.dot(hn_ref[...], w_up_ref[:, cols], preferred_element_type=_f32), 0.0)
        z_ref[:, cols] = (z * z).astype(_bf16)
    proj_ref[:, 0:D_MODEL] = jnp.dot(z_ref[...], w_down_ref[...], preferred_element_type=_f32)

    gfin = gfin_ref[...]
    for _, r in chunks:
        x2 = x1_ref[r, :] + gt_f * proj_ref[r, 0:D_MODEL]
        o_ref[r, :] = x2 * _rms_scale(x2) * gfin


def _gate_block_diag(w):
    per_half = GATE_HALF // LRU_HEAD_DIM
    w4 = w.reshape(2, per_half, LRU_HEAD_DIM, LRU_HEAD_DIM)
    eye = jnp.eye(per_half, dtype=w.dtype)
    return jnp.einsum('hiab,ij->hiajb', w4, eye).reshape(2, GATE_HALF, GATE_HALF)


def _const_spec(shape):
    zeros = (0,) * len(shape)
    return pl.BlockSpec(shape, lambda b, j: zeros, pipeline_mode=pl.Buffered(1))


def kernel(x, c, w_ada, b_ada, g_mix, w_in, conv_w_sc, conv_w_lru, conv_b_lru, w_rg_a, b_rg_a,
           w_rg_x, b_rg_x, lru_lambda, w_out, g_mlp, w_up, w_down, g_final):
    bsz, seq, _ = x.shape
    assert w_ada.shape[0] == 1 and seq % SEQ_TILE == 0
    t = SEQ_TILE

    mod = _adaln_mod(c, w_ada[0], b_ada[0].reshape(1, -1)).reshape(bsz, N_MOD, D_MODEL)

    wg = jnp.concatenate([_gate_block_diag(w_rg_a[0]), _gate_block_diag(w_rg_x[0])],
                         axis=-1).astype(_bf16)
    bg = jnp.concatenate([b_rg_a[0].reshape(2, 1, GATE_HALF), b_rg_x[0].reshape(2, 1, GATE_HALF)],
                         axis=-1)

    operands = (
        x, mod, g_mix, w_in[0].astype(_bf16), conv_w_sc[0], conv_w_lru[0], conv_b_lru,
        wg, bg, lru_lambda, w_out[0].astype(_bf16), g_mlp, w_up[0].astype(_bf16),
        w_down[0].astype(_bf16), g_final.reshape(1, D_MODEL),
    )
    in_specs = [
        pl.BlockSpec((None, t, D_MODEL), lambda b, j: (b, j, 0)),
        pl.BlockSpec((None, N_MOD, D_MODEL), lambda b, j: (b, 0, 0)),
    ] + [_const_spec(op.shape) for op in operands[2:]]

    return pl.pallas_call(
        _block_kernel,
        grid=(bsz, seq // t),
        in_specs=in_specs,
        out_specs=pl.BlockSpec((None, t, D_MODEL), lambda b, j: (b, j, 0)),
        out_shape=jax.ShapeDtypeStruct(x.shape, x.dtype),
        scratch_shapes=[
            pltpu.VMEM((t, D_MODEL), _bf16),
            pltpu.VMEM((t, D_IN), _f32),
            pltpu.VMEM((SUBLANES, SC_WIDTH), _f32),
            pltpu.VMEM((SUBLANES, LRU_WIDTH), _f32),
            pltpu.VMEM((t, LRU_WIDTH), _f32),
            pltpu.VMEM((t, LRU_WIDTH), _bf16),
            pltpu.VMEM((t, D_MODEL), _bf16),
            pltpu.VMEM((t, D_MODEL), _f32),
            pltpu.VMEM((t, D_FF), _bf16),
            pltpu.VMEM((SUBLANES, LRU_WIDTH), _f32),
        ],
        compiler_params=pltpu.CompilerParams(
            dimension_semantics=("arbitrary", "arbitrary"),
            vmem_limit_bytes=VMEM_LIMIT_BYTES,
        ),
        name="hybrid_block",
    )(*operands)
```

```python
import functools

import jax
import jax.numpy as jnp
from jax import lax
from jax.experimental import pallas as pl
from jax.experimental.pallas import tpu as pltpu

D_MODEL = 1024
SC_WIDTH = 512
LRU_WIDTH = 512
LRU_HEADS = 8
LRU_HEAD_DIM = 64
SC_CONV_W = 3
LRU_CONV_W = 4
RG_C = 8.0
D_FF = 4096
N_MOD = 6
EPS = 1e-6
D_IN = 3 * SC_WIDTH + 2 * LRU_WIDTH

SUBLANES = 8
LANES = 128
GATE_HALF = 256
SEQ_TILE = 512
ROW_CHUNK = 64
FF_CHUNK = 1024
DOWN_CHUNK = 512
VMEM_LIMIT_BYTES = 58 * 1024 * 1024

_f32 = jnp.float32
_bf16 = jnp.bfloat16
_u32 = jnp.uint32


def _ada_kernel(c_ref, w_ref, b_ref, o_ref):
    c = c_ref[...]
    c_act = c * jax.nn.sigmoid(c)
    o_ref[...] = jnp.dot(c_act, w_ref[...], preferred_element_type=_f32) + b_ref[...]


def _adaln_mod(c, w_ada, b_ada):
    bsz = c.shape[0]
    return pl.pallas_call(
        _ada_kernel,
        grid=(N_MOD,),
        in_specs=[
            pl.BlockSpec((bsz, D_MODEL), lambda i: (0, 0)),
            pl.BlockSpec((D_MODEL, D_MODEL), lambda i: (0, i)),
            pl.BlockSpec((1, D_MODEL), lambda i: (0, i)),
        ],
        out_specs=pl.BlockSpec((bsz, D_MODEL), lambda i: (0, i)),
        out_shape=jax.ShapeDtypeStruct((bsz, N_MOD * D_MODEL), _f32),
        name="adaln_mod",
    )(c, w_ada, b_ada)


def _rms_scale(v):
    return lax.rsqrt(jnp.mean(v * v, axis=-1, keepdims=True) + EPS)


def _shifted_rows(prev, cur, max_shift):
    groups = cur.shape[0] // SUBLANES
    width = cur.shape[1]
    ext = jnp.concatenate([prev, cur], axis=0).reshape(groups + 1, SUBLANES, width)
    sub = lax.broadcasted_iota(jnp.int32, (groups, SUBLANES, width), 1)
    out = {}
    for m in range(1, max_shift + 1):
        rot = pltpu.roll(ext, m, axis=1)
        out[m] = jnp.where(sub >= m, rot[1:], rot[:-1]).reshape(cur.shape)
    return out


def _token(v):
    u = pltpu.bitcast(v[v.shape[0] - SUBLANES:, v.shape[1] - LANES:], _u32)
    return pltpu.bitcast((u >> 16) >> 16, _f32)


def _after(tok, v):
    head = v[0:SUBLANES, :] + jnp.concatenate([tok] * (v.shape[1] // LANES), axis=1)
    if v.shape[0] == SUBLANES:
        return head
    return jnp.concatenate([head, v[SUBLANES:, :]], axis=0)


def _hold_until(tok, ref):
    tile = ref[0:2 * SUBLANES, 0:LANES]
    ref[0:2 * SUBLANES, 0:LANES] = tile + pltpu.bitcast(tok, _bf16)


_CHUNKS = [(i * ROW_CHUNK, pl.ds(i * ROW_CHUNK, ROW_CHUNK)) for i in range(SEQ_TILE // ROW_CHUNK)]


def _block_kernel(tiles_per_seq,
                  xa_ref, xb_ref, moda_ref, modb_ref, gmix_ref, w_in_ref, cwsc_ref, cwlru_ref,
                  cblru_ref, wg_ref, bg_ref, lam_ref, w_out_ref, gmlp_ref, w_up_ref, w_down_ref,
                  gfin_ref, o_ref,
                  hn_a_ref, proj_ref, ptail_ref, utail_ref, u_ref, ub_ref, hcar_ref,
                  y_ref, hn_b_ref, z_ref, dn_ref):
    t = SEQ_TILE
    s = pl.program_id(0)
    n_tiles = pl.num_programs(0) - 1
    tile_a = jnp.minimum(s, n_tiles - 1)
    first_of_seq = lax.rem(tile_a, tiles_per_seq) == 0

    @pl.when(s == 0)
    def _():
        y_ref[...] = jnp.zeros_like(y_ref)

    @pl.when(first_of_seq)
    def _():
        ptail_ref[...] = jnp.zeros_like(ptail_ref)
        utail_ref[...] = jnp.zeros_like(utail_ref)
        hcar_ref[...] = jnp.zeros_like(hcar_ref)

    mod_a = moda_ref[...]
    sh_m, sc_m = mod_a[0:1, :], mod_a[1:2, :]
    mod_b = modb_ref[...]
    gt_m, sh_f, sc_f, gt_f = mod_b[2:3, :], mod_b[3:4, :], mod_b[4:5, :], mod_b[5:6, :]
    o1, o2, o3, o4 = SC_WIDTH, 2 * SC_WIDTH, 3 * SC_WIDTH, 3 * SC_WIDTH + LRU_WIDTH

    res = jnp.dot(y_ref[...], w_out_ref[...], preferred_element_type=_f32)
    o_ref[...] = res

    scale_m = gmix_ref[...] * (1.0 + sc_m)
    for _, r in _CHUNKS:
        xv = xa_ref[r, :]
        hn_a_ref[r, :] = (xv * _rms_scale(xv) * scale_m + sh_m).astype(_bf16)

    res = jnp.dot(hn_a_ref[...], w_in_ref[...], preferred_element_type=_f32)
    proj_ref[...] = res

    scale_f = gmlp_ref[...] * (1.0 + sc_f)
    for _, r in _CHUNKS:
        x1 = xb_ref[r, :] + gt_m * o_ref[r, :]
        o_ref[r, :] = x1
        hn_b_ref[r, :] = (x1 * _rms_scale(x1) * scale_f + sh_f).astype(_bf16)

    def up_chunk(c):
        cols = slice(c * FF_CHUNK, (c + 1) * FF_CHUNK)
        up = jnp.dot(hn_b_ref[...], w_up_ref[:, cols], preferred_element_type=_f32)
        z = jnp.maximum(up, 0.0)
        z_ref[:, cols] = (z * z).astype(_bf16)
        return _token(up)

    tok_up0 = up_chunk(0)

    cw = cwsc_ref[...]
    prev = ptail_ref[...]
    for _, r in _CHUNKS:
        p = proj_ref[r, o1:o2] * proj_ref[r, o2:o3]
        back = _shifted_rows(prev, p, SC_CONV_W - 1)
        conv = cw[SC_CONV_W - 1:SC_CONV_W, :] * p
        for k in range(SC_CONV_W - 1):
            conv = conv + cw[k:k + 1, :] * back[SC_CONV_W - 1 - k]
        y_sc = proj_ref[r, 0:o1] * conv
        y_ref[r, 0:SC_WIDTH] = y_sc.astype(_bf16)
        prev = p[ROW_CHUNK - SUBLANES:, :]
    ptail_ref[...] = prev
    tok_sc = _token(y_sc)

    _hold_until(tok_sc, hn_b_ref)
    tok_up1 = up_chunk(1)

    cl = cwlru_ref[...]
    prev = utail_ref[...]
    for _, r in _CHUNKS:
        lx = _after(tok_up0, proj_ref[r, o4:D_IN])
        back = _shifted_rows(prev, lx, LRU_CONV_W - 1)
        u = cblru_ref[...] + cl[LRU_CONV_W - 1:LRU_CONV_W, :] * lx
        for k in range(LRU_CONV_W - 1):
            u = u + cl[k:k + 1, :] * back[LRU_CONV_W - 1 - k]
        u_ref[r, :] = u
        ub_ref[r, :] = u.astype(_bf16)
        prev = lx[ROW_CHUNK - SUBLANES:, :]
    utail_ref[...] = prev
    tok_conv = _token(u)

    for half in range(2):
        c0 = half * GATE_HALF
        proj_ref[:, 2 * c0:2 * c0 + 2 * GATE_HALF] = jnp.dot(
            ub_ref[:, c0:c0 + GATE_HALF], wg_ref[half], preferred_element_type=_f32)

    neg_lam = -lam_ref[...]
    softplus = jnp.maximum(neg_lam, 0.0) + jnp.log1p(jnp.exp(-jnp.abs(neg_lam)))
    decay = -RG_C * softplus

    groups = ROW_CHUNK // SUBLANES
    sub = lax.broadcasted_iota(jnp.int32, (groups, SUBLANES, GATE_HALF), 1)

    def gate_chunks(chunk_list, tok):
        for _, r in chunk_list:
            for half in range(2):
                c0 = half * GATE_HALF
                g0 = 2 * c0
                bias = bg_ref[half]
                pre_r = _after(tok, proj_ref[r, g0:g0 + GATE_HALF])
                r_gate = jax.nn.sigmoid(pre_r + bias[:, 0:GATE_HALF])
                i_gate = jax.nn.sigmoid(
                    proj_ref[r, g0 + GATE_HALF:g0 + 2 * GATE_HALF]
                    + bias[:, GATE_HALF:2 * GATE_HALF])
                log_a = r_gate * decay[:, c0:c0 + GATE_HALF]
                a = jnp.exp(log_a)
                mult = jnp.sqrt(-jnp.tanh(log_a) * (a * a + 1.0))
                b = mult * (i_gate * u_ref[r, c0:c0 + GATE_HALF])
                a3 = a.reshape(groups, SUBLANES, GATE_HALF)
                b3 = b.reshape(groups, SUBLANES, GATE_HALF)
                for d in (1, 2, 4):
                    keep = sub >= d
                    a_prev = pltpu.roll(a3, d, axis=1)
                    b_prev = pltpu.roll(b3, d, axis=1)
                    b3 = jnp.where(keep, a3 * b_prev + b3, b3)
                    a3 = jnp.where(keep, a3 * a_prev, a3)
                b_out = b3.reshape(ROW_CHUNK, GATE_HALF)
                proj_ref[r, o2 + c0:o2 + c0 + GATE_HALF] = a3.reshape(ROW_CHUNK, GATE_HALF)
                proj_ref[r, o4 + c0:o4 + c0 + GATE_HALF] = b_out
        return _token(b_out)

    _hold_until(tok_conv, hn_b_ref)
    tok_up2 = up_chunk(2)
    tok_g0 = gate_chunks(_CHUNKS[:len(_CHUNKS) // 2], tok_up1)

    _hold_until(tok_g0, hn_b_ref)
    tok_up3 = up_chunk(3)
    tok_g1 = gate_chunks(_CHUNKS[len(_CHUNKS) // 2:], tok_up2)

    def down_pass(c):
        cols = slice(c * DOWN_CHUNK, (c + 1) * DOWN_CHUNK)
        dn_ref[:, cols] = jnp.dot(z_ref[...], w_down_ref[:, cols], preferred_element_type=_f32)

    _hold_until(tok_g1, z_ref)
    down_pass(0)

    h_in = hcar_ref[...]
    for g in range(t // SUBLANES):
        rows = pl.ds(g * SUBLANES, SUBLANES)
        a_g = proj_ref[rows, o2:o3]
        if g == 0:
            a_g = _after(tok_up3, a_g)
        h = a_g * h_in + proj_ref[rows, o4:D_IN]
        proj_ref[rows, o4:D_IN] = h
        h_in = jnp.broadcast_to(h[SUBLANES - 1:SUBLANES, :], (SUBLANES, LRU_WIDTH))
    hcar_ref[...] = h_in

    for _, r in _CHUNKS:
        y_lru = jax.nn.gelu(proj_ref[r, o3:o4], approximate=True) * proj_ref[r, o4:D_IN]
        y_ref[r, SC_WIDTH:D_MODEL] = y_lru.astype(_bf16)
    tok_y = _token(y_lru)

    _hold_until(tok_y, z_ref)
    down_pass(1)

    gfin = gfin_ref[...]
    for _, r in _CHUNKS:
        x2 = o_ref[r, :] + gt_f * dn_ref[r, :]
        o_ref[r, :] = x2 * _rms_scale(x2) * gfin


def _gate_block_diag(w):
    per_half = GATE_HALF // LRU_HEAD_DIM
    w4 = w.reshape(2, per_half, LRU_HEAD_DIM, LRU_HEAD_DIM)
    eye = jnp.eye(per_half, dtype=w.dtype)
    return jnp.einsum('hiab,ij->hiajb', w4, eye).reshape(2, GATE_HALF, GATE_HALF)


def _const_spec(shape):
    zeros = (0,) * len(shape)
    return pl.BlockSpec(shape, lambda s: zeros, pipeline_mode=pl.Buffered(1))


def kernel(x, c, w_ada, b_ada, g_mix, w_in, conv_w_sc, conv_w_lru, conv_b_lru, w_rg_a, b_rg_a,
           w_rg_x, b_rg_x, lru_lambda, w_out, g_mlp, w_up, w_down, g_final):
    bsz, seq, _ = x.shape
    assert w_ada.shape[0] == 1 and seq % SEQ_TILE == 0
    t = SEQ_TILE
    tiles_per_seq = seq // t
    n_tiles = bsz * tiles_per_seq

    mod = _adaln_mod(c, w_ada[0], b_ada[0].reshape(1, -1)).reshape(bsz, N_MOD, D_MODEL)

    wg = jnp.concatenate([_gate_block_diag(w_rg_a[0]), _gate_block_diag(w_rg_x[0])],
                         axis=-1).astype(_bf16)
    bg = jnp.concatenate([b_rg_a[0].reshape(2, 1, GATE_HALF), b_rg_x[0].reshape(2, 1, GATE_HALF)],
                         axis=-1)

    def tile_a(s):
        return jnp.minimum(s, n_tiles - 1)

    def tile_b(s):
        return jnp.maximum(s - 1, 0)

    x2d = x.reshape(bsz * seq, D_MODEL)
    weights = (
        g_mix, w_in[0].astype(_bf16), conv_w_sc[0], conv_w_lru[0], conv_b_lru, wg, bg, lru_lambda,
        w_out[0].astype(_bf16), g_mlp, w_up[0].astype(_bf16), w_down[0].astype(_bf16),
        g_final.reshape(1, D_MODEL),
    )
    in_specs = [
        pl.BlockSpec((t, D_MODEL), lambda s: (tile_a(s), 0)),
        pl.BlockSpec((t, D_MODEL), lambda s: (tile_b(s), 0)),
        pl.BlockSpec((None, N_MOD, D_MODEL), lambda s: (tile_a(s) // tiles_per_seq, 0, 0)),
        pl.BlockSpec((None, N_MOD, D_MODEL), lambda s: (tile_b(s) // tiles_per_seq, 0, 0)),
    ] + [_const_spec(w.shape) for w in weights]

    out = pl.pallas_call(
        functools.partial(_block_kernel, tiles_per_seq),
        grid=(n_tiles + 1,),
        in_specs=in_specs,
        out_specs=pl.BlockSpec((t, D_MODEL), lambda s: (tile_b(s), 0)),
        out_shape=jax.ShapeDtypeStruct(x2d.shape, x.dtype),
        scratch_shapes=[
            pltpu.VMEM((t, D_MODEL), _bf16),
            pltpu.VMEM((t, D_IN), _f32),
            pltpu.VMEM((SUBLANES, SC_WIDTH), _f32),
            pltpu.VMEM((SUBLANES, LRU_WIDTH), _f32),
            pltpu.VMEM((t, LRU_WIDTH), _f32),
            pltpu.VMEM((t, LRU_WIDTH), _bf16),
            pltpu.VMEM((SUBLANES, LRU_WIDTH), _f32),
            pltpu.VMEM((t, D_MODEL), _bf16),
            pltpu.VMEM((t, D_MODEL), _bf16),
            pltpu.VMEM((t, D_FF), _bf16),
            pltpu.VMEM((t, D_MODEL), _f32),
        ],
        compiler_params=pltpu.CompilerParams(
            dimension_semantics=("arbitrary",),
            vmem_limit_bytes=VMEM_LIMIT_BYTES,
        ),
        name="hybrid_block",
    )(x2d, x2d, mod, mod, *weights)
    return out.reshape(x.shape)
```

```python
import functools

import jax
import jax.numpy as jnp
from jax import lax
from jax.experimental import pallas as pl
from jax.experimental.pallas import tpu as pltpu

D_MODEL = 1024
SC_WIDTH = 512
LRU_WIDTH = 512
LRU_HEADS = 8
LRU_HEAD_DIM = 64
SC_CONV_W = 3
LRU_CONV_W = 4
RG_C = 8.0
D_FF = 4096
N_MOD = 6
EPS = 1e-6
D_IN = 3 * SC_WIDTH + 2 * LRU_WIDTH

SUBLANES = 8
LANES = 128
GATE_HALF = 256
SEQ_TILE = 512
ROW_CHUNK = 64
FF_CHUNK = 1024
DOWN_CHUNK = 512
VMEM_LIMIT_BYTES = 58 * 1024 * 1024

_f32 = jnp.float32
_bf16 = jnp.bfloat16
_u32 = jnp.uint32


def _ada_kernel(c_ref, w_ref, b_ref, o_ref):
    c = c_ref[...]
    c_act = c * jax.nn.sigmoid(c)
    o_ref[...] = jnp.dot(c_act, w_ref[...], preferred_element_type=_f32) + b_ref[...]


def _adaln_mod(c, w_ada, b_ada):
    bsz = c.shape[0]
    return pl.pallas_call(
        _ada_kernel,
        grid=(N_MOD,),
        in_specs=[
            pl.BlockSpec((bsz, D_MODEL), lambda i: (0, 0)),
            pl.BlockSpec((D_MODEL, D_MODEL), lambda i: (0, i)),
            pl.BlockSpec((1, D_MODEL), lambda i: (0, i)),
        ],
        out_specs=pl.BlockSpec((bsz, D_MODEL), lambda i: (0, i)),
        out_shape=jax.ShapeDtypeStruct((bsz, N_MOD * D_MODEL), _f32),
        name="adaln_mod",
    )(c, w_ada, b_ada)


def _rms_scale(v):
    return lax.rsqrt(jnp.mean(v * v, axis=-1, keepdims=True) + EPS)


def _shifted_rows(prev, cur, max_shift):
    groups = cur.shape[0] // SUBLANES
    width = cur.shape[1]
    ext = jnp.concatenate([prev, cur], axis=0).reshape(groups + 1, SUBLANES, width)
    sub = lax.broadcasted_iota(jnp.int32, (groups, SUBLANES, width), 1)
    out = {}
    for m in range(1, max_shift + 1):
        rot = pltpu.roll(ext, m, axis=1)
        out[m] = jnp.where(sub >= m, rot[1:], rot[:-1]).reshape(cur.shape)
    return out


def _token(v):
    u = pltpu.bitcast(v[v.shape[0] - SUBLANES:, v.shape[1] - LANES:], _u32)
    return pltpu.bitcast((u >> 16) >> 16, _f32)


def _after(tok, v):
    head = v[0:SUBLANES, :] + jnp.concatenate([tok] * (v.shape[1] // LANES), axis=1)
    if v.shape[0] == SUBLANES:
        return head
    return jnp.concatenate([head, v[SUBLANES:, :]], axis=0)


def _hold_until(tok, ref):
    tile = ref[0:2 * SUBLANES, 0:LANES]
    ref[0:2 * SUBLANES, 0:LANES] = tile + pltpu.bitcast(tok, _bf16)


_CHUNKS = [(i * ROW_CHUNK, pl.ds(i * ROW_CHUNK, ROW_CHUNK)) for i in range(SEQ_TILE // ROW_CHUNK)]


def _block_kernel(tiles_per_seq,
                  xa_ref, xb_ref, moda_ref, modb_ref, gmix_ref, w_in_ref, cwsc_ref, cwlru_ref,
                  cblru_ref, wg_ref, bg_ref, lam_ref, w_out_ref, gmlp_ref, w_up_ref, w_down_ref,
                  gfin_ref, o_ref,
                  hn_a_ref, proj_ref, ptail_ref, utail_ref, u_ref, ub_ref, hcar_ref,
                  y_ref, hn_b_ref, z_ref, dn_ref, gpre_ref, gy_ref, ab_ref):
    t = SEQ_TILE
    s = pl.program_id(0)
    n_tiles = pl.num_programs(0) - 1
    tile_a = jnp.minimum(s, n_tiles - 1)
    tile_b = jnp.maximum(s - 1, 0)

    @pl.when(s == 0)
    def _():
        y_ref[...] = jnp.zeros_like(y_ref)
        gpre_ref[...] = jnp.zeros_like(gpre_ref)
        gy_ref[...] = jnp.zeros_like(gy_ref)
        u_ref[...] = jnp.zeros_like(u_ref)

    @pl.when(lax.rem(tile_a, tiles_per_seq) == 0)
    def _():
        ptail_ref[...] = jnp.zeros_like(ptail_ref)
        utail_ref[...] = jnp.zeros_like(utail_ref)

    @pl.when(lax.rem(tile_b, tiles_per_seq) == 0)
    def _():
        hcar_ref[...] = jnp.zeros_like(hcar_ref)

    mod_a = moda_ref[...]
    sh_m, sc_m = mod_a[0:1, :], mod_a[1:2, :]
    mod_b = modb_ref[...]
    gt_m, sh_f, sc_f, gt_f = mod_b[2:3, :], mod_b[3:4, :], mod_b[4:5, :], mod_b[5:6, :]
    o1, o2, o3, o4 = SC_WIDTH, 2 * SC_WIDTH, 3 * SC_WIDTH, 3 * SC_WIDTH + LRU_WIDTH

    scale_m = gmix_ref[...] * (1.0 + sc_m)
    for _, r in _CHUNKS:
        xv = xa_ref[r, :]
        hn = xv * _rms_scale(xv) * scale_m + sh_m
        hn_a_ref[r, :] = hn.astype(_bf16)
    tok_norm = _token(hn)
    proj_ref[:, 0:o3] = jnp.dot(hn_a_ref[...], w_in_ref[:, 0:o3], preferred_element_type=_f32)

    neg_lam = -lam_ref[...]
    softplus = jnp.maximum(neg_lam, 0.0) + jnp.log1p(jnp.exp(-jnp.abs(neg_lam)))
    decay = -RG_C * softplus

    groups = ROW_CHUNK // SUBLANES
    sub = lax.broadcasted_iota(jnp.int32, (groups, SUBLANES, GATE_HALF), 1)
    for _, r in _CHUNKS:
        for half in range(2):
            c0 = half * GATE_HALF
            g0 = 2 * c0
            bias = bg_ref[half]
            pre_r = _after(tok_norm, gpre_ref[r, g0:g0 + GATE_HALF])
            r_gate = jax.nn.sigmoid(pre_r + bias[:, 0:GATE_HALF])
            i_gate = jax.nn.sigmoid(
                gpre_ref[r, g0 + GATE_HALF:g0 + 2 * GATE_HALF] + bias[:, GATE_HALF:2 * GATE_HALF])
            log_a = r_gate * decay[:, c0:c0 + GATE_HALF]
            a = jnp.exp(log_a)
            mult = jnp.sqrt(-jnp.tanh(log_a) * (a * a + 1.0))
            b = mult * (i_gate * u_ref[r, c0:c0 + GATE_HALF])
            a3 = a.reshape(groups, SUBLANES, GATE_HALF)
            b3 = b.reshape(groups, SUBLANES, GATE_HALF)
            for d in (1, 2, 4):
                keep = sub >= d
                a_prev = pltpu.roll(a3, d, axis=1)
                b_prev = pltpu.roll(b3, d, axis=1)
                b3 = jnp.where(keep, a3 * b_prev + b3, b3)
                a3 = jnp.where(keep, a3 * a_prev, a3)
            ab_ref[r, c0:c0 + GATE_HALF] = a3.reshape(ROW_CHUNK, GATE_HALF)
            ab_ref[r, LRU_WIDTH + c0:LRU_WIDTH + c0 + GATE_HALF] = b3.reshape(ROW_CHUNK, GATE_HALF)

    h_in = hcar_ref[...]
    for g in range(t // SUBLANES):
        rows = pl.ds(g * SUBLANES, SUBLANES)
        h = ab_ref[rows, 0:LRU_WIDTH] * h_in + ab_ref[rows, LRU_WIDTH:2 * LRU_WIDTH]
        y_lru = gy_ref[rows, :] * h
        y_ref[rows, SC_WIDTH:D_MODEL] = y_lru.astype(_bf16)
        h_in = jnp.broadcast_to(h[SUBLANES - 1:SUBLANES, :], (SUBLANES, LRU_WIDTH))
    hcar_ref[...] = h_in
    tok_y = _token(y_lru)

    res = jnp.dot(y_ref[...], w_out_ref[...], preferred_element_type=_f32)
    o_ref[...] = res

    cw = cwsc_ref[...]
    prev = ptail_ref[...]
    for _, r in _CHUNKS:
        p = _after(tok_y, proj_ref[r, o1:o2]) * proj_ref[r, o2:o3]
        back = _shifted_rows(prev, p, SC_CONV_W - 1)
        conv = cw[SC_CONV_W - 1:SC_CONV_W, :] * p
        for k in range(SC_CONV_W - 1):
            conv = conv + cw[k:k + 1, :] * back[SC_CONV_W - 1 - k]
        y_sc = proj_ref[r, 0:o1] * conv
        y_ref[r, 0:SC_WIDTH] = y_sc.astype(_bf16)
        prev = p[ROW_CHUNK - SUBLANES:, :]
    ptail_ref[...] = prev
    tok_sc = _token(y_sc)

    _hold_until(tok_sc, hn_a_ref)
    res = jnp.dot(hn_a_ref[...], w_in_ref[:, o3:D_IN], preferred_element_type=_f32)
    proj_ref[:, o3:D_IN] = res
    tok_in = _token(res)

    scale_f = gmlp_ref[...] * (1.0 + sc_f)
    for _, r in _CHUNKS:
        x1 = xb_ref[r, :] + gt_m * o_ref[r, :]
        o_ref[r, :] = x1
        hn_b_ref[r, :] = (x1 * _rms_scale(x1) * scale_f + sh_f).astype(_bf16)

    def up_chunk(c):
        cols = slice(c * FF_CHUNK, (c + 1) * FF_CHUNK)
        up = jnp.dot(hn_b_ref[...], w_up_ref[:, cols], preferred_element_type=_f32)
        z = jnp.maximum(up, 0.0)
        z_ref[:, cols] = (z * z).astype(_bf16)
        return _token(up)

    up_chunk(0)

    cl = cwlru_ref[...]
    prev = utail_ref[...]
    for _, r in _CHUNKS:
        lx = _after(tok_in, proj_ref[r, o4:D_IN])
        back = _shifted_rows(prev, lx, LRU_CONV_W - 1)
        u = cblru_ref[...] + cl[LRU_CONV_W - 1:LRU_CONV_W, :] * lx
        for k in range(LRU_CONV_W - 1):
            u = u + cl[k:k + 1, :] * back[LRU_CONV_W - 1 - k]
        u_ref[r, :] = u
        ub_ref[r, :] = u.astype(_bf16)
        gy_ref[r, :] = jax.nn.gelu(proj_ref[r, o3:o4], approximate=True)
        prev = lx[ROW_CHUNK - SUBLANES:, :]
    utail_ref[...] = prev
    tok_conv = _token(u)

    _hold_until(tok_conv, hn_b_ref)
    up_chunk(1)
    for half in range(2):
        c0 = half * GATE_HALF
        res = jnp.dot(ub_ref[:, c0:c0 + GATE_HALF], wg_ref[half], preferred_element_type=_f32)
        gpre_ref[:, 2 * c0:2 * c0 + 2 * GATE_HALF] = res
    tok_gate = _token(res)

    _hold_until(tok_gate, hn_b_ref)
    up_chunk(2)
    up_chunk(3)
    for c in range(D_MODEL // DOWN_CHUNK):
        cols = slice(c * DOWN_CHUNK, (c + 1) * DOWN_CHUNK)
        dn_ref[:, cols] = jnp.dot(z_ref[...], w_down_ref[:, cols], preferred_element_type=_f32)

    gfin = gfin_ref[...]
    for _, r in _CHUNKS:
        x2 = o_ref[r, :] + gt_f * dn_ref[r, :]
        o_ref[r, :] = x2 * _rms_scale(x2) * gfin


def _gate_block_diag(w):
    per_half = GATE_HALF // LRU_HEAD_DIM
    w4 = w.reshape(2, per_half, LRU_HEAD_DIM, LRU_HEAD_DIM)
    eye = jnp.eye(per_half, dtype=w.dtype)
    return jnp.einsum('hiab,ij->hiajb', w4, eye).reshape(2, GATE_HALF, GATE_HALF)


def _const_spec(shape):
    zeros = (0,) * len(shape)
    return pl.BlockSpec(shape, lambda s: zeros, pipeline_mode=pl.Buffered(1))


def kernel(x, c, w_ada, b_ada, g_mix, w_in, conv_w_sc, conv_w_lru, conv_b_lru, w_rg_a, b_rg_a,
           w_rg_x, b_rg_x, lru_lambda, w_out, g_mlp, w_up, w_down, g_final):
    bsz, seq, _ = x.shape
    assert w_ada.shape[0] == 1 and seq % SEQ_TILE == 0
    t = SEQ_TILE
    tiles_per_seq = seq // t
    n_tiles = bsz * tiles_per_seq

    mod = _adaln_mod(c, w_ada[0], b_ada[0].reshape(1, -1)).reshape(bsz, N_MOD, D_MODEL)

    wg = jnp.concatenate([_gate_block_diag(w_rg_a[0]), _gate_block_diag(w_rg_x[0])],
                         axis=-1).astype(_bf16)
    bg = jnp.concatenate([b_rg_a[0].reshape(2, 1, GATE_HALF), b_rg_x[0].reshape(2, 1, GATE_HALF)],
                         axis=-1)

    def tile_a(s):
        return jnp.minimum(s, n_tiles - 1)

    def tile_b(s):
        return jnp.maximum(s - 1, 0)

    x2d = x.reshape(bsz * seq, D_MODEL)
    weights = (
        g_mix, w_in[0].astype(_bf16), conv_w_sc[0], conv_w_lru[0], conv_b_lru, wg, bg, lru_lambda,
        w_out[0].astype(_bf16), g_mlp, w_up[0].astype(_bf16), w_down[0].astype(_bf16),
        g_final.reshape(1, D_MODEL),
    )
    in_specs = [
        pl.BlockSpec((t, D_MODEL), lambda s: (tile_a(s), 0)),
        pl.BlockSpec((t, D_MODEL), lambda s: (tile_b(s), 0)),
        pl.BlockSpec((None, N_MOD, D_MODEL), lambda s: (tile_a(s) // tiles_per_seq, 0, 0)),
        pl.BlockSpec((None, N_MOD, D_MODEL), lambda s: (tile_b(s) // tiles_per_seq, 0, 0)),
    ] + [_const_spec(w.shape) for w in weights]

    out = pl.pallas_call(
        functools.partial(_block_kernel, tiles_per_seq),
        grid=(n_tiles + 1,),
        in_specs=in_specs,
        out_specs=pl.BlockSpec((t, D_MODEL), lambda s: (tile_b(s), 0)),
        out_shape=jax.ShapeDtypeStruct(x2d.shape, x.dtype),
        scratch_shapes=[
            pltpu.VMEM((t, D_MODEL), _bf16),
            pltpu.VMEM((t, D_IN), _f32),
            pltpu.VMEM((SUBLANES, SC_WIDTH), _f32),
            pltpu.VMEM((SUBLANES, LRU_WIDTH), _f32),
            pltpu.VMEM((t, LRU_WIDTH), _f32),
            pltpu.VMEM((t, LRU_WIDTH), _bf16),
            pltpu.VMEM((SUBLANES, LRU_WIDTH), _f32),
            pltpu.VMEM((t, D_MODEL), _bf16),
            pltpu.VMEM((t, D_MODEL), _bf16),
            pltpu.VMEM((t, D_FF), _bf16),
            pltpu.VMEM((t, D_MODEL), _f32),
            pltpu.VMEM((t, 4 * GATE_HALF), _f32),
            pltpu.VMEM((t, LRU_WIDTH), _f32),
            pltpu.VMEM((t, 2 * LRU_WIDTH), _f32),
        ],
        compiler_params=pltpu.CompilerParams(
            dimension_semantics=("arbitrary",),
            vmem_limit_bytes=VMEM_LIMIT_BYTES,
        ),
        name="hybrid_block",
    )(x2d, x2d, mod, mod, *weights)
    return out.reshape(x.shape)
```

```python
import functools

import jax
import jax.numpy as jnp
from jax import lax
from jax.experimental import pallas as pl
from jax.experimental.pallas import tpu as pltpu

D_MODEL = 1024
SC_WIDTH = 512
LRU_WIDTH = 512
LRU_HEADS = 8
LRU_HEAD_DIM = 64
SC_CONV_W = 3
LRU_CONV_W = 4
RG_C = 8.0
D_FF = 4096
N_MOD = 6
EPS = 1e-6
D_IN = 3 * SC_WIDTH + 2 * LRU_WIDTH

SUBLANES = 8
LANES = 128
GATE_HALF = 256
SEQ_TILE = 512
ROW_CHUNK = 64
FF_CHUNK = 1024
DOWN_CHUNK = 512
VMEM_LIMIT_BYTES = 58 * 1024 * 1024

_f32 = jnp.float32
_bf16 = jnp.bfloat16
_u32 = jnp.uint32


def _ada_kernel(c_ref, w_ref, b_ref, o_ref):
    c = c_ref[...]
    c_act = c * jax.nn.sigmoid(c)
    o_ref[...] = jnp.dot(c_act, w_ref[...], preferred_element_type=_f32) + b_ref[...]


def _adaln_mod(c, w_ada, b_ada):
    bsz = c.shape[0]
    return pl.pallas_call(
        _ada_kernel,
        grid=(N_MOD,),
        in_specs=[
            pl.BlockSpec((bsz, D_MODEL), lambda i: (0, 0)),
            pl.BlockSpec((D_MODEL, D_MODEL), lambda i: (0, i)),
            pl.BlockSpec((1, D_MODEL), lambda i: (0, i)),
        ],
        out_specs=pl.BlockSpec((bsz, D_MODEL), lambda i: (0, i)),
        out_shape=jax.ShapeDtypeStruct((bsz, N_MOD * D_MODEL), _f32),
        name="adaln_mod",
    )(c, w_ada, b_ada)


def _rms_scale(v):
    return lax.rsqrt(jnp.mean(v * v, axis=-1, keepdims=True) + EPS)


def _shifted_rows(prev, cur, max_shift):
    groups = cur.shape[0] // SUBLANES
    width = cur.shape[1]
    ext = jnp.concatenate([prev, cur], axis=0).reshape(groups + 1, SUBLANES, width)
    sub = lax.broadcasted_iota(jnp.int32, (groups, SUBLANES, width), 1)
    out = {}
    for m in range(1, max_shift + 1):
        rot = pltpu.roll(ext, m, axis=1)
        out[m] = jnp.where(sub >= m, rot[1:], rot[:-1]).reshape(cur.shape)
    return out


def _token(v):
    u = pltpu.bitcast(v[v.shape[0] - SUBLANES:, v.shape[1] - LANES:], _u32)
    return pltpu.bitcast((u >> 16) >> 16, _f32)


def _after(tok, v):
    head = v[0:SUBLANES, :] + jnp.concatenate([tok] * (v.shape[1] // LANES), axis=1)
    if v.shape[0] == SUBLANES:
        return head
    return jnp.concatenate([head, v[SUBLANES:, :]], axis=0)


def _after_all(tok, v):
    row = jnp.concatenate([tok] * (v.shape[1] // LANES), axis=1)
    return v + jnp.concatenate([row] * (v.shape[0] // SUBLANES), axis=0)


def _hold_until(tok, ref):
    tile = ref[0:2 * SUBLANES, 0:LANES]
    ref[0:2 * SUBLANES, 0:LANES] = tile + pltpu.bitcast(tok, _bf16)


_CHUNKS = [(i * ROW_CHUNK, pl.ds(i * ROW_CHUNK, ROW_CHUNK)) for i in range(SEQ_TILE // ROW_CHUNK)]


def _block_kernel(tiles_per_seq,
                  xa_ref, xb_ref, moda_ref, modb_ref, gmix_ref, w_in_ref, cwsc_ref, cwlru_ref,
                  cblru_ref, wg_ref, bg_ref, lam_ref, w_out_ref, gmlp_ref, w_up_ref, w_down_ref,
                  gfin_ref, o_ref,
                  hn_a_ref, proj_ref, ptail_ref, utail_ref, u_ref, ub_ref, hcar_ref,
                  y_ref, hn_b_ref, z_ref, dn_ref, gpre_ref, gy_ref, ab_ref):
    t = SEQ_TILE
    s = pl.program_id(0)
    n_tiles = pl.num_programs(0) - 1
    tile_a = jnp.minimum(s, n_tiles - 1)
    tile_b = jnp.maximum(s - 1, 0)

    @pl.when(s == 0)
    def _():
        y_ref[...] = jnp.zeros_like(y_ref)
        ab_ref[...] = jnp.zeros_like(ab_ref)
        gy_ref[...] = jnp.zeros_like(gy_ref)

    @pl.when(lax.rem(tile_a, tiles_per_seq) == 0)
    def _():
        ptail_ref[...] = jnp.zeros_like(ptail_ref)
        utail_ref[...] = jnp.zeros_like(utail_ref)

    @pl.when(lax.rem(tile_b, tiles_per_seq) == 0)
    def _():
        hcar_ref[...] = jnp.zeros_like(hcar_ref)

    mod_a = moda_ref[...]
    sh_m, sc_m = mod_a[0:1, :], mod_a[1:2, :]
    mod_b = modb_ref[...]
    gt_m, sh_f, sc_f, gt_f = mod_b[2:3, :], mod_b[3:4, :], mod_b[4:5, :], mod_b[5:6, :]
    o1, o2, o3, o4 = SC_WIDTH, 2 * SC_WIDTH, 3 * SC_WIDTH, 3 * SC_WIDTH + LRU_WIDTH

    scale_m = gmix_ref[...] * (1.0 + sc_m)
    for _, r in _CHUNKS:
        xv = xa_ref[r, :]
        hn = xv * _rms_scale(xv) * scale_m + sh_m
        hn_a_ref[r, :] = hn.astype(_bf16)
    tok_norm = _token(hn)
    proj_ref[:, 0:o3] = jnp.dot(hn_a_ref[...], w_in_ref[:, 0:o3], preferred_element_type=_f32)

    h_in = hcar_ref[...]
    for g in range(t // SUBLANES):
        rows = pl.ds(g * SUBLANES, SUBLANES)
        a_g = ab_ref[rows, 0:LRU_WIDTH]
        if g == 0:
            a_g = _after(tok_norm, a_g)
        h = a_g * h_in + ab_ref[rows, LRU_WIDTH:2 * LRU_WIDTH]
        y_lru = gy_ref[rows, :] * h
        y_ref[rows, SC_WIDTH:D_MODEL] = y_lru.astype(_bf16)
        h_in = jnp.broadcast_to(h[SUBLANES - 1:SUBLANES, :], (SUBLANES, LRU_WIDTH))
    hcar_ref[...] = h_in
    tok_y = _token(y_lru)

    res = jnp.dot(y_ref[...], w_out_ref[...], preferred_element_type=_f32)
    o_ref[...] = res

    cw = cwsc_ref[...]
    prev = ptail_ref[...]
    for _, r in _CHUNKS:
        p = _after(tok_y, proj_ref[r, o1:o2]) * proj_ref[r, o2:o3]
        back = _shifted_rows(prev, p, SC_CONV_W - 1)
        conv = cw[SC_CONV_W - 1:SC_CONV_W, :] * p
        for k in range(SC_CONV_W - 1):
            conv = conv + cw[k:k + 1, :] * back[SC_CONV_W - 1 - k]
        y_sc = proj_ref[r, 0:o1] * conv
        y_ref[r, 0:SC_WIDTH] = y_sc.astype(_bf16)
        prev = p[ROW_CHUNK - SUBLANES:, :]
    ptail_ref[...] = prev
    tok_sc = _token(y_sc)

    _hold_until(tok_sc, hn_a_ref)
    res = jnp.dot(hn_a_ref[...], w_in_ref[:, o3:D_IN], preferred_element_type=_f32)
    proj_ref[:, o3:D_IN] = res
    tok_in = _token(res)

    scale_f = gmlp_ref[...] * (1.0 + sc_f)
    for _, r in _CHUNKS:
        x1 = xb_ref[r, :] + gt_m * o_ref[r, :]
        o_ref[r, :] = x1
        hn_b_ref[r, :] = (x1 * _rms_scale(x1) * scale_f + sh_f).astype(_bf16)

    def up_chunk(c):
        cols = slice(c * FF_CHUNK, (c + 1) * FF_CHUNK)
        up = jnp.dot(hn_b_ref[...], w_up_ref[:, cols], preferred_element_type=_f32)
        z = jnp.maximum(up, 0.0)
        z_ref[:, cols] = (z * z).astype(_bf16)
        return _token(up)

    up_chunk(0)

    cl = cwlru_ref[...]
    prev = utail_ref[...]
    for _, r in _CHUNKS:
        lx = _after(tok_in, proj_ref[r, o4:D_IN])
        back = _shifted_rows(prev, lx, LRU_CONV_W - 1)
        u = cblru_ref[...] + cl[LRU_CONV_W - 1:LRU_CONV_W, :] * lx
        for k in range(LRU_CONV_W - 1):
            u = u + cl[k:k + 1, :] * back[LRU_CONV_W - 1 - k]
        u_ref[r, :] = u
        ub_ref[r, :] = u.astype(_bf16)
        gy_ref[r, :] = jax.nn.gelu(proj_ref[r, o3:o4], approximate=True)
        prev = lx[ROW_CHUNK - SUBLANES:, :]
    utail_ref[...] = prev
    tok_conv = _token(u)

    _hold_until(tok_conv, hn_b_ref)
    up_chunk(1)
    for half in range(2):
        c0 = half * GATE_HALF
        res = jnp.dot(ub_ref[:, c0:c0 + GATE_HALF], wg_ref[half], preferred_element_type=_f32)
        gpre_ref[:, 2 * c0:2 * c0 + 2 * GATE_HALF] = res
    tok_gate = _token(res)

    _hold_until(tok_gate, hn_b_ref)
    up_chunk(2)
    tok_up3 = up_chunk(3)

    def down_pass(c):
        cols = slice(c * DOWN_CHUNK, (c + 1) * DOWN_CHUNK)
        dn_ref[:, cols] = jnp.dot(z_ref[...], w_down_ref[:, cols], preferred_element_type=_f32)

    down_pass(0)

    neg_lam = -lam_ref[...]
    softplus = jnp.maximum(neg_lam, 0.0) + jnp.log1p(jnp.exp(-jnp.abs(neg_lam)))
    decay = -RG_C * softplus

    groups = ROW_CHUNK // SUBLANES
    sub = lax.broadcasted_iota(jnp.int32, (groups, SUBLANES, GATE_HALF), 1)
    for _, r in _CHUNKS:
        for half in range(2):
            c0 = half * GATE_HALF
            g0 = 2 * c0
            bias = bg_ref[half]
            pre_r = _after_all(tok_up3, gpre_ref[r, g0:g0 + GATE_HALF])
            r_gate = jax.nn.sigmoid(pre_r + bias[:, 0:GATE_HALF])
            i_gate = jax.nn.sigmoid(
                gpre_ref[r, g0 + GATE_HALF:g0 + 2 * GATE_HALF] + bias[:, GATE_HALF:2 * GATE_HALF])
            log_a = r_gate * decay[:, c0:c0 + GATE_HALF]
            a = jnp.exp(log_a)
            mult = jnp.sqrt(-jnp.tanh(log_a) * (a * a + 1.0))
            b = mult * (i_gate * u_ref[r, c0:c0 + GATE_HALF])
            a3 = a.reshape(groups, SUBLANES, GATE_HALF)
            b3 = b.reshape(groups, SUBLANES, GATE_HALF)
            for d in (1, 2, 4):
                keep = sub >= d
                a_prev = pltpu.roll(a3, d, axis=1)
                b_prev = pltpu.roll(b3, d, axis=1)
                b3 = jnp.where(keep, a3 * b_prev + b3, b3)
                a3 = jnp.where(keep, a3 * a_prev, a3)
            ab_ref[r, c0:c0 + GATE_HALF] = a3.reshape(ROW_CHUNK, GATE_HALF)
            b_out = b3.reshape(ROW_CHUNK, GATE_HALF)
            ab_ref[r, LRU_WIDTH + c0:LRU_WIDTH + c0 + GATE_HALF] = b_out

    _hold_until(_token(b_out), z_ref)
    down_pass(1)

    gfin = gfin_ref[...]
    for _, r in _CHUNKS:
        x2 = o_ref[r, :] + gt_f * dn_ref[r, :]
        o_ref[r, :] = x2 * _rms_scale(x2) * gfin


def _gate_block_diag(w):
    per_half = GATE_HALF // LRU_HEAD_DIM
    w4 = w.reshape(2, per_half, LRU_HEAD_DIM, LRU_HEAD_DIM)
    eye = jnp.eye(per_half, dtype=w.dtype)
    return jnp.einsum('hiab,ij->hiajb', w4, eye).reshape(2, GATE_HALF, GATE_HALF)


def _const_spec(shape):
    zeros = (0,) * len(shape)
    return pl.BlockSpec(shape, lambda s: zeros, pipeline_mode=pl.Buffered(1))


def kernel(x, c, w_ada, b_ada, g_mix, w_in, conv_w_sc, conv_w_lru, conv_b_lru, w_rg_a, b_rg_a,
           w_rg_x, b_rg_x, lru_lambda, w_out, g_mlp, w_up, w_down, g_final):
    bsz, seq, _ = x.shape
    assert w_ada.shape[0] == 1 and seq % SEQ_TILE == 0
    t = SEQ_TILE
    tiles_per_seq = seq // t
    n_tiles = bsz * tiles_per_seq

    mod = _adaln_mod(c, w_ada[0], b_ada[0].reshape(1, -1)).reshape(bsz, N_MOD, D_MODEL)

    wg = jnp.concatenate([_gate_block_diag(w_rg_a[0]), _gate_block_diag(w_rg_x[0])],
                         axis=-1).astype(_bf16)
    bg = jnp.concatenate([b_rg_a[0].reshape(2, 1, GATE_HALF), b_rg_x[0].reshape(2, 1, GATE_HALF)],
                         axis=-1)

    def tile_a(s):
        return jnp.minimum(s, n_tiles - 1)

    def tile_b(s):
        return jnp.maximum(s - 1, 0)

    x2d = x.reshape(bsz * seq, D_MODEL)
    weights = (
        g_mix, w_in[0].astype(_bf16), conv_w_sc[0], conv_w_lru[0], conv_b_lru, wg, bg, lru_lambda,
        w_out[0].astype(_bf16), g_mlp, w_up[0].astype(_bf16), w_down[0].astype(_bf16),
        g_final.reshape(1, D_MODEL),
    )
    in_specs = [
        pl.BlockSpec((t, D_MODEL), lambda s: (tile_a(s), 0)),
        pl.BlockSpec((t, D_MODEL), lambda s: (tile_b(s), 0)),
        pl.BlockSpec((None, N_MOD, D_MODEL), lambda s: (tile_a(s) // tiles_per_seq, 0, 0)),
        pl.BlockSpec((None, N_MOD, D_MODEL), lambda s: (tile_b(s) // tiles_per_seq, 0, 0)),
    ] + [_const_spec(w.shape) for w in weights]

    out = pl.pallas_call(
        functools.partial(_block_kernel, tiles_per_seq),
        grid=(n_tiles + 1,),
        in_specs=in_specs,
        out_specs=pl.BlockSpec((t, D_MODEL), lambda s: (tile_b(s), 0)),
        out_shape=jax.ShapeDtypeStruct(x2d.shape, x.dtype),
        scratch_shapes=[
            pltpu.VMEM((t, D_MODEL), _bf16),
            pltpu.VMEM((t, D_IN), _f32),
            pltpu.VMEM((SUBLANES, SC_WIDTH), _f32),
            pltpu.VMEM((SUBLANES, LRU_WIDTH), _f32),
            pltpu.VMEM((t, LRU_WIDTH), _f32),
            pltpu.VMEM((t, LRU_WIDTH), _bf16),
            pltpu.VMEM((SUBLANES, LRU_WIDTH), _f32),
            pltpu.VMEM((t, D_MODEL), _bf16),
            pltpu.VMEM((t, D_MODEL), _bf16),
            pltpu.VMEM((t, D_FF), _bf16),
            pltpu.VMEM((t, D_MODEL), _f32),
            pltpu.VMEM((t, 4 * GATE_HALF), _f32),
            pltpu.VMEM((t, LRU_WIDTH), _f32),
            pltpu.VMEM((t, 2 * LRU_WIDTH), _f32),
        ],
        compiler_params=pltpu.CompilerParams(
            dimension_semantics=("arbitrary",),
            vmem_limit_bytes=VMEM_LIMIT_BYTES,
        ),
        name="hybrid_block",
    )(x2d, x2d, mod, mod, *weights)
    return out.reshape(x.shape)
```
